```python
import jax
import jax.numpy as jnp
from jax import lax
import numpy as np

D_MODEL = 4096
BATCH = 2
SEQ = 4096
DEPTH = 2

CTX_LEN = 256
GRID_W = 64
HEAD_DIM = 128
N_MLSTM = D_MODEL // (4 * HEAD_DIM)
N_SGU = D_MODEL // (4 * HEAD_DIM)
N_NA = D_MODEL // (2 * HEAD_DIM)
D_MLSTM = N_MLSTM * HEAD_DIM
D_SGU = N_SGU * HEAD_DIM
D_NA = N_NA * HEAD_DIM
D_MIX = D_MLSTM + D_SGU + D_NA
D_IN = 4 * D_MLSTM + 4 * N_MLSTM + 2 * D_SGU + 3 * D_NA
MLSTM_CHUNK = 128
SGU_CHUNK = 128
QK_CONV = 3
NA_ROWS = 8
NA_COLS = 16
FFN_CONV = 3
D_FF = 256 * ((8 * D_MODEL) // (3 * 256))
ROPE_THETA = 10000.0
EPS = 1e-6
F_BIAS_LO = 3.0
F_BIAS_HI = 6.0

_IN_LAYOUT = (('m_q', D_MLSTM), ('m_k', D_MLSTM), ('m_v', D_MLSTM), ('m_o', D_MLSTM),
              ('m_gate', 4 * N_MLSTM), ('s_u', D_SGU), ('s_v', D_SGU),
              ('n_q', D_NA), ('n_k', D_NA), ('n_v', D_NA))
_CTX_KV_NAMES = ('m_k', 'm_v', 'm_gate', 'n_k', 'n_v')

kernel_name = 'hybrid_mlstm_sgu_natten_dit'


def _col_ranges():
    out, start = {}, 0
    for name, width in _IN_LAYOUT:
        out[name] = (start, start + width)
        start += width
    return out


def _split_cols(z):
    return {nm: z[..., a:b] for nm, (a, b) in _col_ranges().items()}


def _project_cols(h, w, names):
    cols = _col_ranges()
    return {nm: h @ w[:, cols[nm][0]:cols[nm][1]] for nm in names}


def _rmsnorm(x, g):
    xf = x.astype(jnp.float32)
    y = xf * lax.rsqrt(jnp.mean(xf * xf, axis=-1, keepdims=True) + EPS)
    return (y * g.astype(jnp.float32)).astype(x.dtype)


def _modulate(h, shift, scale):
    return h * (1 + scale[:, None, :]) + shift[:, None, :]


def _dwconv(x, w):
    K, C = w.shape
    return lax.conv_general_dilated(x, w[:, None, :].astype(x.dtype), window_strides=(1,),
                                    padding=[(K // 2, K // 2)],
                                    dimension_numbers=('NWC', 'WIO', 'NWC'),
                                    feature_group_count=C)


def _heads(z):
    B, T, _ = z.shape
    return z.reshape(B, T, -1, HEAD_DIM).transpose(0, 2, 1, 3)


def _rev(a):
    return jnp.flip(a, axis=2)


def _axial_rope(T):
    t = jnp.arange(T)
    row = (t // GRID_W).astype(jnp.float32)
    col = (t % GRID_W).astype(jnp.float32)
    n_freq = HEAD_DIM // 4
    inv_freq = ROPE_THETA ** (-jnp.arange(n_freq, dtype=jnp.float32) / n_freq)
    ang = jnp.concatenate([row[:, None] * inv_freq, col[:, None] * inv_freq], axis=-1)
    return jnp.cos(ang), jnp.sin(ang)


def _rope(x, cos, sin):
    x1, x2 = x[..., :HEAD_DIM // 2], x[..., HEAD_DIM // 2:]
    return jnp.concatenate([x1 * cos - x2 * sin, x2 * cos + x1 * sin], axis=-1)


def _mlstm_chunkwise(q, k, v, li, lf, state):
    B, H, T, d = k.shape
    nc = T // MLSTM_CHUNK

    def chunks(a):
        return jnp.moveaxis(a.reshape(B, H, nc, MLSTM_CHUNK, *a.shape[3:]), 2, 0)

    causal = jnp.tril(jnp.ones((MLSTM_CHUNK, MLSTM_CHUNK), dtype=bool))

    def step(carry, xs):
        C, n, m = carry
        qc, kc, vc, lic, lfc = xs
        b = jnp.cumsum(lfc, axis=-1)
        d_log = b[..., :, None] - b[..., None, :] + lic[..., None, :]
        d_log = jnp.where(causal, d_log, -jnp.inf)
        inter = b + m[..., None]
        m_t = jnp.maximum(inter, d_log.max(axis=-1))
        a = jnp.exp(inter - m_t)
        s = jnp.einsum('bhtd,bhsd->bhts', qc, kc) * jnp.exp(d_log - m_t[..., None])
        num = jnp.einsum('bhts,bhse->bhte', s, vc) + a[..., None] * jnp.einsum('bhtd,bhde->bhte', qc, C)
        den = s.sum(axis=-1) + a * jnp.einsum('bhtd,bhd->bht', qc, n)
        h = num / jnp.maximum(jnp.abs(den), jnp.exp(-m_t))[..., None]
        b_end = b[..., -1]
        g = b_end[..., None] - b + lic
        m_new = jnp.maximum(b_end + m, g.max(axis=-1))
        decay = jnp.exp(b_end + m - m_new)
        w = jnp.exp(g - m_new[..., None])
        C_new = decay[..., None, None] * C + jnp.einsum('bhs,bhsd,bhse->bhde', w, kc, vc)
        n_new = decay[..., None] * n + jnp.einsum('bhs,bhsd->bhd', w, kc)
        return (C_new, n_new, m_new), h

    state, h = lax.scan(step, state, (chunks(q), chunks(k), chunks(v), chunks(li), chunks(lf)))
    return jnp.moveaxis(h, 0, 2).reshape(B, H, T, d), state


def _mlstm_final_state(k, v, li, lf):
    b = jnp.cumsum(lf, axis=-1)
    b_end = b[..., -1]
    g = b_end[..., None] - b + li
    m = jnp.maximum(b_end, g.max(axis=-1))
    w = jnp.exp(g - m[..., None])
    return (jnp.einsum('bht,bhtd,bhte->bhde', w, k, v), jnp.einsum('bht,bhtd->bhd', w, k), m)


def _mlstm_streams(z, conv_q, conv_k, gate_b, with_q, rotary):
    f32 = jnp.float32
    B, T, _ = z['m_k'].shape
    k = _heads(jax.nn.silu(_dwconv(z['m_k'], conv_k))).astype(f32) * HEAD_DIM ** -0.5
    v = _heads(z['m_v']).astype(f32)
    q = _heads(jax.nn.silu(_dwconv(z['m_q'], conv_q))).astype(f32) if with_q else None
    if rotary:
        cos, sin = _axial_rope(T)
        q, k = _rope(q, cos, sin), _rope(k, cos, sin)
    g = z['m_gate'].astype(f32).reshape(B, T, 4, N_MLSTM) + gate_b.astype(f32)
    g = jnp.transpose(g, (2, 0, 3, 1))
    return q, k, v, (g[0], jax.nn.log_sigmoid(g[1]), g[2], jax.nn.log_sigmoid(g[3]))


def _mlstm_out(h, zo, norm_g):
    B, H, T, d = h.shape
    h = h * lax.rsqrt(jnp.mean(h * h, axis=-1, keepdims=True) + EPS) * norm_g.astype(jnp.float32).reshape(H, 1, d)
    h = h.transpose(0, 2, 1, 3).reshape(B, T, H * d)
    return (jax.nn.sigmoid(zo.astype(jnp.float32)) * h).astype(zo.dtype)


def _mlstm_mixer(zl, zc, conv_q, conv_k, gate_b, norm_g, ctx_out):
    f32 = jnp.float32
    qc, kc, vc, (ci_f, cf_f, ci_b, cf_b) = _mlstm_streams(zc, conv_q, conv_k, gate_b, ctx_out, False)
    out_c = None
    if ctx_out:
        B, H, _, d = kc.shape
        zero = (jnp.zeros((B, H, d, d), f32), jnp.zeros((B, H, d), f32), jnp.zeros((B, H), f32))
        hf, st_f = _mlstm_chunkwise(qc, kc, vc, ci_f, cf_f, zero)
        hb, st_b = _mlstm_chunkwise(_rev(qc), _rev(kc), _rev(vc), _rev(ci_b), _rev(cf_b), zero)
        out_c = _mlstm_out(hf + _rev(hb), zc['m_o'], norm_g)
    else:
        st_f = _mlstm_final_state(kc, vc, ci_f, cf_f)
        st_b = _mlstm_final_state(_rev(kc), _rev(vc), _rev(ci_b), _rev(cf_b))
    q, k, v, (i_f, f_f, i_b, f_b) = _mlstm_streams(zl, conv_q, conv_k, gate_b, True, True)
    hf, _ = _mlstm_chunkwise(q, k, v, i_f, f_f, st_f)
    hb, _ = _mlstm_chunkwise(_rev(q), _rev(k), _rev(v), _rev(i_b), _rev(f_b), st_b)
    out_l = _mlstm_out(hf + _rev(hb), zl['m_o'], norm_g)
    return out_l, out_c


def _sgu(zu, zv, norm_g, w_s, b_s):
    B, T, _ = zu.shape
    u = jax.nn.gelu(zu)
    v = _rmsnorm(jax.nn.gelu(zv), norm_g)
    vh = v.reshape(B, T // SGU_CHUNK, SGU_CHUNK, N_SGU, HEAD_DIM)
    mixed = jnp.einsum('gpq,bcqgd->bcpgd', w_s, vh) + jnp.transpose(b_s)[:, :, None]
    return u * mixed.reshape(B, T, D_SGU)


def _na_mixer(zl, zc, rpb, ctx_out):
    f32 = jnp.float32
    B, T, _ = zl['n_q'].shape
    rows = T // GRID_W
    kr = min(NA_ROWS, rows)
    scale = HEAD_DIM ** -0.5
    qg = zl['n_q'].reshape(B, rows, GRID_W, N_NA, HEAD_DIM) * scale
    kg = zl['n_k'].reshape(B, rows, GRID_W, N_NA, HEAD_DIM)
    vg = zl['n_v'].reshape(B, rows, GRID_W, N_NA, HEAD_DIM)
    kc = zc['n_k'].reshape(B, -1, N_NA, HEAD_DIM)
    vc = zc['n_v'].reshape(B, -1, N_NA, HEAD_DIM)
    col = jnp.arange(GRID_W)
    cstart = jnp.clip(col - NA_COLS // 2, 0, GRID_W - NA_COLS)
    col_ok = (col[None, :] >= cstart[:, None]) & (col[None, :] < cstart[:, None] + NA_COLS)
    col_rel = jnp.clip(col[None, :] - col[:, None] + NA_COLS - 1, 0, 2 * NA_COLS - 2)
    rpb_cols = rpb.astype(f32)[:, :, col_rel]
    n_loc = kr * GRID_W

    def row_block(r):
        rs = jnp.clip(r - kr // 2, 0, rows - kr)
        qr = lax.dynamic_index_in_dim(qg, r, axis=1, keepdims=False)
        kb = lax.dynamic_slice_in_dim(kg, rs, kr, axis=1)
        vb = lax.dynamic_slice_in_dim(vg, rs, kr, axis=1)
        bias = jnp.transpose(rpb_cols[:, rs + jnp.arange(kr) - r + NA_ROWS - 1], (0, 2, 1, 3))
        s_loc = jnp.einsum('bqhd,bkwhd->bhqkw', qr, kb).astype(f32) + bias
        s_loc = jnp.where(col_ok[:, None, :], s_loc, -jnp.inf).reshape(B, N_NA, GRID_W, n_loc)
        s_ctx = jnp.einsum('bqhd,bchd->bhqc', qr, kc).astype(f32)
        p = jax.nn.softmax(jnp.concatenate([s_loc, s_ctx], axis=-1), axis=-1).astype(vg.dtype)
        o = jnp.einsum('bhqk,bkhd->bqhd', p[..., :n_loc], vb.reshape(B, n_loc, N_NA, HEAD_DIM))
        return o + jnp.einsum('bhqc,bchd->bqhd', p[..., n_loc:], vc)

    out = lax.map(row_block, jnp.arange(rows))
    out_l = jnp.moveaxis(out, 0, 1).reshape(B, T, D_NA)
    out_c = None
    if ctx_out:
        qc = zc['n_q'].reshape(B, -1, N_NA, HEAD_DIM) * scale
        p = jax.nn.softmax(jnp.einsum('bqhd,bkhd->bhqk', qc, kc).astype(f32), axis=-1).astype(vc.dtype)
        out_c = jnp.einsum('bhqk,bkhd->bqhd', p, vc).reshape(B, -1, D_NA)
    return out_l, out_c


def _conv_ffn(h, w_gate, w_up, conv, w_down):
    a = _dwconv(h @ w_gate, conv)
    return (jax.nn.silu(a) * (h @ w_up)) @ w_down


def setup_inputs(seed: int = 0) -> dict:
    key = jax.random.key(seed)
    ks = jax.random.split(key, 23)
    f32 = jnp.float32

    def nrm(k, shape, scale):
        return scale * jax.random.normal(k, shape, f32)

    f_base = jnp.linspace(F_BIAS_LO, F_BIAS_HI, N_MLSTM, dtype=f32)
    zeros_h = jnp.zeros((N_MLSTM,), f32)
    gate_base = jnp.stack([zeros_h, f_base, zeros_h, f_base])
    return {
        'x': nrm(ks[0], (BATCH, SEQ, D_MODEL), 1.0),
        'c': nrm(ks[1], (BATCH, D_MODEL), 1.0),
        'ctx': nrm(ks[2], (BATCH, CTX_LEN, D_MODEL), 1.0),
        'c_ctx': nrm(ks[3], (D_MODEL,), 1.0),
        'ada_w': nrm(ks[4], (DEPTH, D_MODEL, 6 * D_MODEL), D_MODEL ** -0.5),
        'ada_b': nrm(ks[5], (DEPTH, 6 * D_MODEL), 0.01),
        'mix_norm_g': 1.0 + nrm(ks[6], (DEPTH, D_MODEL), 0.02),
        'ffn_norm_g': 1.0 + nrm(ks[7], (DEPTH, D_MODEL), 0.02),
        'w_in': nrm(ks[8], (DEPTH, D_MODEL, D_IN), D_MODEL ** -0.5),
        'mlstm_conv_q': nrm(ks[9], (DEPTH, QK_CONV, D_MLSTM), QK_CONV ** -0.5),
        'mlstm_conv_k': nrm(ks[10], (DEPTH, QK_CONV, D_MLSTM), QK_CONV ** -0.5),
        'mlstm_gate_b': gate_base + nrm(ks[11], (DEPTH, 4, N_MLSTM), 0.1),
        'mlstm_norm_g': 1.0 + nrm(ks[12], (DEPTH, D_MLSTM), 0.02),
        'sgu_norm_g': 1.0 + nrm(ks[13], (DEPTH, D_SGU), 0.02),
        'sgu_w': nrm(ks[14], (DEPTH, N_SGU, SGU_CHUNK, SGU_CHUNK), SGU_CHUNK ** -0.5),
        'sgu_b': 1.0 + nrm(ks[15], (DEPTH, N_SGU, SGU_CHUNK), 0.02),
        'na_rpb': nrm(ks[16], (DEPTH, N_NA, 2 * NA_ROWS - 1, 2 * NA_COLS - 1), 0.1),
        'w_out': nrm(ks[17], (DEPTH, D_MIX, D_MODEL), D_MIX ** -0.5),
        'ffn_w_gate': nrm(ks[18], (DEPTH, D_MODEL, D_FF), D_MODEL ** -0.5),
        'ffn_w_up': nrm(ks[19], (DEPTH, D_MODEL, D_FF), D_MODEL ** -0.5),
        'ffn_conv': nrm(ks[20], (DEPTH, FFN_CONV, D_FF), FFN_CONV ** -0.5),
        'ffn_w_down': nrm(ks[21], (DEPTH, D_FF, D_MODEL), D_FF ** -0.5),
        'final_norm_g': 1.0 + nrm(ks[22], (D_MODEL,), 0.02),
    }


def reference(x, c, ctx, c_ctx, ada_w, ada_b, mix_norm_g, ffn_norm_g, w_in, mlstm_conv_q, mlstm_conv_k,
              mlstm_gate_b, mlstm_norm_g, sgu_norm_g, sgu_w, sgu_b, na_rpb, w_out, ffn_w_gate, ffn_w_up,
              ffn_conv, ffn_w_down, final_norm_g):
    xl, xc = x, ctx
    silu_c = jax.nn.silu(c)
    silu_cc = jax.nn.silu(c_ctx)[None, :]
    for l in range(DEPTH):
        ctx_out = l < DEPTH - 1
        sh1, sc1, g1, sh2, sc2, g2 = jnp.split(silu_c @ ada_w[l] + ada_b[l], 6, axis=-1)
        csh1, csc1, cg1, csh2, csc2, cg2 = jnp.split(silu_cc @ ada_w[l] + ada_b[l], 6, axis=-1)
        hl = _modulate(_rmsnorm(xl, mix_norm_g[l]), sh1, sc1)
        hc = _modulate(_rmsnorm(xc, mix_norm_g[l]), csh1, csc1)
        zl = _split_cols(hl @ w_in[l])
        zc = _split_cols(hc @ w_in[l]) if ctx_out else _project_cols(hc, w_in[l], _CTX_KV_NAMES)
        m_l, m_c = _mlstm_mixer(zl, zc, mlstm_conv_q[l], mlstm_conv_k[l], mlstm_gate_b[l], mlstm_norm_g[l], ctx_out)
        n_l, n_c = _na_mixer(zl, zc, na_rpb[l], ctx_out)
        s_l = _sgu(zl['s_u'], zl['s_v'], sgu_norm_g[l], sgu_w[l], sgu_b[l])
        xl = xl + g1[:, None, :] * (jnp.concatenate([m_l, s_l, n_l], axis=-1) @ w_out[l])
        hl = _modulate(_rmsnorm(xl, ffn_norm_g[l]), sh2, sc2)
        xl = xl + g2[:, None, :] * _conv_ffn(hl, ffn_w_gate[l], ffn_w_up[l], ffn_conv[l], ffn_w_down[l])
        if ctx_out:
            s_c = _sgu(zc['s_u'], zc['s_v'], sgu_norm_g[l], sgu_w[l], sgu_b[l])
            xc = xc + cg1[:, None, :] * (jnp.concatenate([m_c, s_c, n_c], axis=-1) @ w_out[l])
            hc = _modulate(_rmsnorm(xc, ffn_norm_g[l]), csh2, csc2)
            xc = xc + cg2[:, None, :] * _conv_ffn(hc, ffn_w_gate[l], ffn_w_up[l], ffn_conv[l], ffn_w_down[l])
    return _rmsnorm(xl, final_norm_g)
```

```python
import functools

import jax
import jax.numpy as jnp
from jax import lax
from jax.experimental import pallas as pl
from jax.experimental.pallas import tpu as pltpu

D_MODEL = 4096
BATCH = 2
SEQ = 4096
DEPTH = 2
CTX_LEN = 256
GRID_W = 64
HEAD_DIM = 128
N_MLSTM = 8
N_SGU = 8
N_NA = 16
D_MLSTM = N_MLSTM * HEAD_DIM
D_SGU = N_SGU * HEAD_DIM
D_NA = N_NA * HEAD_DIM
MLSTM_CHUNK = 128
SGU_CHUNK = 128
NA_ROWS = 8
NA_COLS = 16
D_FF = 10752
ROPE_THETA = 10000.0
EPS = 1e-6

VMEM_LIMIT_BYTES = 56 * 1024 * 1024

F32 = jnp.float32
BF16 = jnp.bfloat16


def _mm_kernel(a_ref, w_ref, o_ref):
    o_ref[...] = jnp.dot(a_ref[...], w_ref[...], preferred_element_type=F32).astype(o_ref.dtype)


def _mm_acc_kernel(a_ref, w_ref, o_ref, acc_ref):
    k = pl.program_id(2)

    @pl.when(k == 0)
    def _():
        acc_ref[...] = jnp.zeros_like(acc_ref)

    acc_ref[...] += jnp.dot(a_ref[...], w_ref[...], preferred_element_type=F32)

    @pl.when(k == pl.num_programs(2) - 1)
    def _():
        o_ref[...] = acc_ref[...].astype(o_ref.dtype)


def _pick(n, cands):
    for c in cands:
        if n % c == 0:
            return c
    return n


def _matmul(a, w, out_dtype=F32, tk=None):
    M, K = a.shape
    _, N = w.shape
    tm = _pick(M, (1024, 512, 256))
    tn = _pick(N, (1024, 512, 256, 128))
    if tk is None or tk == K:
        return pl.pallas_call(
            _mm_kernel,
            grid=(M // tm, N // tn),
            in_specs=[pl.BlockSpec((tm, K), lambda i, j: (i, 0)),
                      pl.BlockSpec((K, tn), lambda i, j: (0, j))],
            out_specs=pl.BlockSpec((tm, tn), lambda i, j: (i, j)),
            out_shape=jax.ShapeDtypeStruct((M, N), out_dtype),
            compiler_params=pltpu.CompilerParams(
                dimension_semantics=("parallel", "parallel"),
                vmem_limit_bytes=VMEM_LIMIT_BYTES),
            name="mm",
        )(a, w)
    return pl.pallas_call(
        _mm_acc_kernel,
        grid=(M // tm, N // tn, K // tk),
        in_specs=[pl.BlockSpec((tm, tk), lambda i, j, k: (i, k)),
                  pl.BlockSpec((tk, tn), lambda i, j, k: (k, j))],
        out_specs=pl.BlockSpec((tm, tn), lambda i, j, k: (i, j)),
        out_shape=jax.ShapeDtypeStruct((M, N), out_dtype),
        scratch_shapes=[pltpu.VMEM((tm, tn), F32)],
        compiler_params=pltpu.CompilerParams(
            dimension_semantics=("parallel", "parallel", "arbitrary"),
            vmem_limit_bytes=VMEM_LIMIT_BYTES),
        name="mm_acc",
    )(a, w)


def _mm3(h, w, out_dtype=F32, tk=None):
    B, T, K = h.shape
    return _matmul(h.reshape(B * T, K).astype(BF16), w, out_dtype, tk).reshape(B, T, -1)


_IN_LAYOUT = (('m_q', D_MLSTM), ('m_k', D_MLSTM), ('m_v', D_MLSTM), ('m_o', D_MLSTM),
              ('m_gate', 4 * N_MLSTM), ('s_u', D_SGU), ('s_v', D_SGU),
              ('n_q', D_NA), ('n_k', D_NA), ('n_v', D_NA))


def _col_ranges():
    out, start = {}, 0
    for name, width in _IN_LAYOUT:
        out[name] = (start, start + width)
        start += width
    return out


def _rmsnorm(x, g):
    xf = x.astype(F32)
    y = xf * lax.rsqrt(jnp.mean(xf * xf, axis=-1, keepdims=True) + EPS)
    return (y * g.astype(F32)).astype(x.dtype)


def _modulate(h, shift, scale):
    return h * (1 + scale[:, None, :]) + shift[:, None, :]


def _dwconv(x, w):
    K, C = w.shape
    return lax.conv_general_dilated(x, w[:, None, :].astype(x.dtype), window_strides=(1,),
                                    padding=[(K // 2, K // 2)],
                                    dimension_numbers=('NWC', 'WIO', 'NWC'),
                                    feature_group_count=C)


def _heads(z):
    B, T, _ = z.shape
    return z.reshape(B, T, -1, HEAD_DIM).transpose(0, 2, 1, 3)


def _rev(a):
    return jnp.flip(a, axis=2)


def _axial_rope(T):
    t = jnp.arange(T)
    row = (t // GRID_W).astype(F32)
    col = (t % GRID_W).astype(F32)
    n_freq = HEAD_DIM // 4
    inv_freq = ROPE_THETA ** (-jnp.arange(n_freq, dtype=F32) / n_freq)
    ang = jnp.concatenate([row[:, None] * inv_freq, col[:, None] * inv_freq], axis=-1)
    return jnp.cos(ang), jnp.sin(ang)


def _rope(x, cos, sin):
    x1, x2 = x[..., :HEAD_DIM // 2], x[..., HEAD_DIM // 2:]
    return jnp.concatenate([x1 * cos - x2 * sin, x2 * cos + x1 * sin], axis=-1)


def _mlstm_chunkwise(q, k, v, li, lf, state):
    B, H, T, d = k.shape
    nc = T // MLSTM_CHUNK

    def chunks(a):
        return jnp.moveaxis(a.reshape(B, H, nc, MLSTM_CHUNK, *a.shape[3:]), 2, 0)

    causal = jnp.tril(jnp.ones((MLSTM_CHUNK, MLSTM_CHUNK), dtype=bool))

    def step(carry, xs):
        C, n, m = carry
        qc, kc, vc, lic, lfc = xs
        b = jnp.cumsum(lfc, axis=-1)
        d_log = b[..., :, None] - b[..., None, :] + lic[..., None, :]
        d_log = jnp.where(causal, d_log, -jnp.inf)
        inter = b + m[..., None]
        m_t = jnp.maximum(inter, d_log.max(axis=-1))
        a = jnp.exp(inter - m_t)
        s = jnp.einsum('bhtd,bhsd->bhts', qc, kc) * jnp.exp(d_log - m_t[..., None])
        num = jnp.einsum('bhts,bhse->bhte', s, vc) + a[..., None] * jnp.einsum('bhtd,bhde->bhte', qc, C)
        den = s.sum(axis=-1) + a * jnp.einsum('bhtd,bhd->bht', qc, n)
        h = num / jnp.maximum(jnp.abs(den), jnp.exp(-m_t))[..., None]
        b_end = b[..., -1]
        g = b_end[..., None] - b + lic
        m_new = jnp.maximum(b_end + m, g.max(axis=-1))
        decay = jnp.exp(b_end + m - m_new)
        w = jnp.exp(g - m_new[..., None])
        C_new = decay[..., None, None] * C + jnp.einsum('bhs,bhsd,bhse->bhde', w, kc, vc)
        n_new = decay[..., None] * n + jnp.einsum('bhs,bhsd->bhd', w, kc)
        return (C_new, n_new, m_new), h

    state, h = lax.scan(step, state, (chunks(q), chunks(k), chunks(v), chunks(li), chunks(lf)))
    return jnp.moveaxis(h, 0, 2).reshape(B, H, T, d), state


def _mlstm_final_state(k, v, li, lf):
    b = jnp.cumsum(lf, axis=-1)
    b_end = b[..., -1]
    g = b_end[..., None] - b + li
    m = jnp.maximum(b_end, g.max(axis=-1))
    w = jnp.exp(g - m[..., None])
    return (jnp.einsum('bht,bhtd,bhte->bhde', w, k, v), jnp.einsum('bht,bhtd->bhd', w, k), m)


def _mlstm_streams(z, conv_q, conv_k, gate_b, with_q, rotary):
    B, T, _ = z['m_k'].shape
    k = _heads(jax.nn.silu(_dwconv(z['m_k'], conv_k))).astype(F32) * HEAD_DIM ** -0.5
    v = _heads(z['m_v']).astype(F32)
    q = _heads(jax.nn.silu(_dwconv(z['m_q'], conv_q))).astype(F32) if with_q else None
    if rotary:
        cos, sin = _axial_rope(T)
        q, k = _rope(q, cos, sin), _rope(k, cos, sin)
    g = z['m_gate'].astype(F32).reshape(B, T, 4, N_MLSTM) + gate_b.astype(F32)
    g = jnp.transpose(g, (2, 0, 3, 1))
    return q, k, v, (g[0], jax.nn.log_sigmoid(g[1]), g[2], jax.nn.log_sigmoid(g[3]))


def _mlstm_out(h, zo, norm_g):
    B, H, T, d = h.shape
    h = h * lax.rsqrt(jnp.mean(h * h, axis=-1, keepdims=True) + EPS) * norm_g.astype(F32).reshape(H, 1, d)
    h = h.transpose(0, 2, 1, 3).reshape(B, T, H * d)
    return (jax.nn.sigmoid(zo.astype(F32)) * h).astype(zo.dtype)


def _mlstm_mixer(zl, zc, conv_q, conv_k, gate_b, norm_g, ctx_out):
    qc, kc, vc, (ci_f, cf_f, ci_b, cf_b) = _mlstm_streams(zc, conv_q, conv_k, gate_b, ctx_out, False)
    out_c = None
    if ctx_out:
        B, H, _, d = kc.shape
        zero = (jnp.zeros((B, H, d, d), F32), jnp.zeros((B, H, d), F32), jnp.zeros((B, H), F32))
        hf, st_f = _mlstm_chunkwise(qc, kc, vc, ci_f, cf_f, zero)
        hb, st_b = _mlstm_chunkwise(_rev(qc), _rev(kc), _rev(vc), _rev(ci_b), _rev(cf_b), zero)
        out_c = _mlstm_out(hf + _rev(hb), zc['m_o'], norm_g)
    else:
        st_f = _mlstm_final_state(kc, vc, ci_f, cf_f)
        st_b = _mlstm_final_state(_rev(kc), _rev(vc), _rev(ci_b), _rev(cf_b))
    q, k, v, (i_f, f_f, i_b, f_b) = _mlstm_streams(zl, conv_q, conv_k, gate_b, True, True)
    hf, _ = _mlstm_chunkwise(q, k, v, i_f, f_f, st_f)
    hb, _ = _mlstm_chunkwise(_rev(q), _rev(k), _rev(v), _rev(i_b), _rev(f_b), st_b)
    out_l = _mlstm_out(hf + _rev(hb), zl['m_o'], norm_g)
    return out_l, out_c


def _sgu(zu, zv, norm_g, w_s, b_s):
    B, T, _ = zu.shape
    u = jax.nn.gelu(zu)
    v = _rmsnorm(jax.nn.gelu(zv), norm_g)
    vh = v.reshape(B, T // SGU_CHUNK, SGU_CHUNK, N_SGU, HEAD_DIM)
    mixed = jnp.einsum('gpq,bcqgd->bcpgd', w_s, vh) + jnp.transpose(b_s)[:, :, None]
    return u * mixed.reshape(B, T, D_SGU)


def _na_mixer(zl, zc, rpb, ctx_out):
    B, T, _ = zl['n_q'].shape
    rows = T // GRID_W
    kr = min(NA_ROWS, rows)
    scale = HEAD_DIM ** -0.5
    qg = zl['n_q'].reshape(B, rows, GRID_W, N_NA, HEAD_DIM) * scale
    kg = zl['n_k'].reshape(B, rows, GRID_W, N_NA, HEAD_DIM)
    vg = zl['n_v'].reshape(B, rows, GRID_W, N_NA, HEAD_DIM)
    kc = zc['n_k'].reshape(B, -1, N_NA, HEAD_DIM)
    vc = zc['n_v'].reshape(B, -1, N_NA, HEAD_DIM)
    col = jnp.arange(GRID_W)
    cstart = jnp.clip(col - NA_COLS // 2, 0, GRID_W - NA_COLS)
    col_ok = (col[None, :] >= cstart[:, None]) & (col[None, :] < cstart[:, None] + NA_COLS)
    col_rel = jnp.clip(col[None, :] - col[:, None] + NA_COLS - 1, 0, 2 * NA_COLS - 2)
    rpb_cols = rpb.astype(F32)[:, :, col_rel]
    n_loc = kr * GRID_W

    def row_block(r):
        rs = jnp.clip(r - kr // 2, 0, rows - kr)
        qr = lax.dynamic_index_in_dim(qg, r, axis=1, keepdims=False)
        kb = lax.dynamic_slice_in_dim(kg, rs, kr, axis=1)
        vb = lax.dynamic_slice_in_dim(vg, rs, kr, axis=1)
        bias = jnp.transpose(rpb_cols[:, rs + jnp.arange(kr) - r + NA_ROWS - 1], (0, 2, 1, 3))
        s_loc = jnp.einsum('bqhd,bkwhd->bhqkw', qr, kb).astype(F32) + bias
        s_loc = jnp.where(col_ok[:, None, :], s_loc, -jnp.inf).reshape(B, N_NA, GRID_W, n_loc)
        s_ctx = jnp.einsum('bqhd,bchd->bhqc', qr, kc).astype(F32)
        p = jax.nn.softmax(jnp.concatenate([s_loc, s_ctx], axis=-1), axis=-1).astype(vg.dtype)
        o = jnp.einsum('bhqk,bkhd->bqhd', p[..., :n_loc], vb.reshape(B, n_loc, N_NA, HEAD_DIM))
        return o + jnp.einsum('bhqc,bchd->bqhd', p[..., n_loc:], vc)

    out = lax.map(row_block, jnp.arange(rows))
    out_l = jnp.moveaxis(out, 0, 1).reshape(B, T, D_NA)
    out_c = None
    if ctx_out:
        qc = zc['n_q'].reshape(B, -1, N_NA, HEAD_DIM) * scale
        p = jax.nn.softmax(jnp.einsum('bqhd,bkhd->bhqk', qc, kc).astype(F32), axis=-1).astype(vc.dtype)
        out_c = jnp.einsum('bhqk,bkhd->bqhd', p, vc).reshape(B, -1, D_NA)
    return out_l, out_c


def _conv_ffn(h, w_gate, w_up, conv, w_down):
    a = _dwconv(_mm3(h, w_gate), conv)
    u = _mm3(h, w_up)
    return _mm3(jax.nn.silu(a) * u, w_down, tk=D_FF // 3)


def kernel(x, c, ctx, c_ctx, ada_w, ada_b, mix_norm_g, ffn_norm_g, w_in, mlstm_conv_q, mlstm_conv_k,
           mlstm_gate_b, mlstm_norm_g, sgu_norm_g, sgu_w, sgu_b, na_rpb, w_out, ffn_w_gate, ffn_w_up,
           ffn_conv, ffn_w_down, final_norm_g):
    xl, xc = x, ctx
    silu_c = jax.nn.silu(c)
    silu_cc = jax.nn.silu(c_ctx)[None, :]
    cols = _col_ranges()
    for l in range(DEPTH):
        ctx_out = l < DEPTH - 1
        sh1, sc1, g1, sh2, sc2, g2 = jnp.split(silu_c @ ada_w[l] + ada_b[l], 6, axis=-1)
        csh1, csc1, cg1, csh2, csc2, cg2 = jnp.split(silu_cc @ ada_w[l] + ada_b[l], 6, axis=-1)
        gate_lo, gate_hi = cols['m_gate']
        w_main = jnp.concatenate([w_in[l][:, :gate_lo], w_in[l][:, gate_hi:]], axis=1).astype(BF16)
        w_gate_cols = w_in[l][:, gate_lo:gate_hi]
        wo = w_out[l].astype(BF16)
        wg = ffn_w_gate[l].astype(BF16)
        wu = ffn_w_up[l].astype(BF16)
        wd = ffn_w_down[l].astype(BF16)

        def project(h):
            zm = _mm3(h, w_main)
            z = {}
            for nm, (a, b) in cols.items():
                if nm == 'm_gate':
                    z[nm] = h @ w_gate_cols
                elif a < gate_lo:
                    z[nm] = zm[..., a:b]
                else:
                    z[nm] = zm[..., a - 32:b - 32]
            return z

        hl = _modulate(_rmsnorm(xl, mix_norm_g[l]), sh1, sc1)
        hc = _modulate(_rmsnorm(xc, mix_norm_g[l]), csh1, csc1)
        zl = project(hl)
        zc = project(hc)
        m_l, m_c = _mlstm_mixer(zl, zc, mlstm_conv_q[l], mlstm_conv_k[l], mlstm_gate_b[l], mlstm_norm_g[l], ctx_out)
        n_l, n_c = _na_mixer(zl, zc, na_rpb[l], ctx_out)
        s_l = _sgu(zl['s_u'], zl['s_v'], sgu_norm_g[l], sgu_w[l], sgu_b[l])
        xl = xl + g1[:, None, :] * _mm3(jnp.concatenate([m_l, s_l, n_l], axis=-1), wo)
        hl = _modulate(_rmsnorm(xl, ffn_norm_g[l]), sh2, sc2)
        xl = xl + g2[:, None, :] * _conv_ffn(hl, wg, wu, ffn_conv[l], wd)
        if ctx_out:
            s_c = _sgu(zc['s_u'], zc['s_v'], sgu_norm_g[l], sgu_w[l], sgu_b[l])
            xc = xc + cg1[:, None, :] * _mm3(jnp.concatenate([m_c, s_c, n_c], axis=-1), wo)
            hc = _modulate(_rmsnorm(xc, ffn_norm_g[l]), csh2, csc2)
            xc = xc + cg2[:, None, :] * _conv_ffn(hc, wg, wu, ffn_conv[l], wd)
    return _rmsnorm(xl, final_norm_g)
```
